```python
import jax, jax.numpy as jnp
from jax import lax
import numpy as np

D_MODEL = 1024
BATCH = 8
SEQ = 2048
DEPTH = 1
DEC_BATCH = 128
DEC_SEQ = 4
PAST_LEN = 8192
PAGE_SIZE = 128

D_MIX = D_MODEL
D_A = D_MIX // 2
CHUNK = 128
GM_GROUPS = 4
GM_GROUP_DIM = D_A // GM_GROUPS
D_B = D_MIX - D_A
HEAD_DIM = 64
N_HEADS = D_B // HEAD_DIM
N_KV_HEADS = 2
KV_REP = N_HEADS // N_KV_HEADS
KV_W = N_KV_HEADS * HEAD_DIM
IDX_HEADS = 8
IDX_DIM = 64
TOPK_MAX = 256
Q_BLOCK = 128
PROJ_SIZES = (D_A, D_A, N_HEADS * HEAD_DIM, KV_W, KV_W, IDX_HEADS * IDX_DIM, IDX_DIM, IDX_HEADS)
P_IN = sum(PROJ_SIZES)
N_MEM = 256
MEM_HEADS = 4
MEM_HEAD_DIM = 128
D_MEM = MEM_HEADS * MEM_HEAD_DIM
PK_HEADS = 8
N_KEYS = 128
N_EXPERTS = N_KEYS * N_KEYS
KEY_DIM = 128
KEY_HALF = KEY_DIM // 2
PK_TOPK = 16
PEER_BLOCK = 128
EPS = 1e-6

kernel_name = 'hybrid_gmlp_dsa_peer_decode_step'


def rmsnorm(x, g):
    xf = x.astype(jnp.float32)
    y = xf * lax.rsqrt(jnp.mean(xf * xf, axis=-1, keepdims=True) + EPS)
    return (y * g.astype(jnp.float32)).astype(x.dtype)


def layernorm(x, g, b):
    xf = x.astype(jnp.float32)
    mu = jnp.mean(xf, axis=-1, keepdims=True)
    xc = xf - mu
    var = jnp.mean(xc * xc, axis=-1, keepdims=True)
    return (xc * lax.rsqrt(var + EPS) * g.astype(jnp.float32) + b.astype(jnp.float32)).astype(x.dtype)


def split_proj(z):
    lead = z.shape[:-1]
    offs, acc = [], 0
    for s in PROJ_SIZES[:-1]:
        acc += s
        offs.append(acc)
    ua, va, q, k, v, qi, ki, wi = jnp.split(z, offs, axis=-1)
    return (ua, va,
            q.reshape(*lead, N_HEADS, HEAD_DIM),
            k.reshape(*lead, N_KV_HEADS, HEAD_DIM),
            v.reshape(*lead, N_KV_HEADS, HEAD_DIM),
            qi.reshape(*lead, IDX_HEADS, IDX_DIM),
            ki, wi)


def gmlp_mix(ua, va, ln_g, ln_b, ws, bs, n):
    Bt, T, _ = ua.shape
    u = jax.nn.gelu(ua)
    vn = layernorm(jax.nn.gelu(va), ln_g, ln_b)
    vc = vn.reshape(Bt, T // n, n, GM_GROUPS, GM_GROUP_DIM)
    w = jnp.where(jnp.tril(jnp.ones((n, n), dtype=bool)), ws[:, :n, :n], 0)
    mixed = jnp.einsum('gts,bcsgd->bctgd', w, vc) + bs[:, :n].T[None, None, :, :, None]
    return u * mixed.reshape(Bt, T, D_A), vn


def index_scores(qi, wi, ki):
    s = jnp.einsum('bthd,bld->bthl', qi, ki, preferred_element_type=jnp.float32)
    w = wi.astype(jnp.float32) * (IDX_HEADS ** -0.5)
    return jnp.einsum('bthl,bth->btl', jax.nn.relu(s), w)


def sparse_attend(q, k_sel, v_sel, valid):
    B, T = q.shape[:2]
    qg = q.reshape(B, T, N_KV_HEADS, KV_REP, HEAD_DIM)
    s = jnp.einsum('btgrd,btkgd->btgrk', qg, k_sel, preferred_element_type=jnp.float32) * (HEAD_DIM ** -0.5)
    s = jnp.where(valid[:, :, None, None, :], s, -jnp.inf)
    p = jax.nn.softmax(s, axis=-1)
    o = jnp.einsum('btgrk,btkgd->btgrd', p.astype(v_sel.dtype), v_sel)
    return o.reshape(B, T, N_HEADS * HEAD_DIM)


def dsa_prompt(q, k, v, qi, ki, wi):
    B, S = q.shape[:2]
    topk = min(TOPK_MAX, S // 4)
    spos = jnp.arange(S)
    bi = jnp.arange(B)[:, None, None]

    def blk(i):
        t0 = i * Q_BLOCK
        qb = lax.dynamic_slice_in_dim(q, t0, Q_BLOCK, axis=1)
        qib = lax.dynamic_slice_in_dim(qi, t0, Q_BLOCK, axis=1)
        wib = lax.dynamic_slice_in_dim(wi, t0, Q_BLOCK, axis=1)
        tpos = t0 + jnp.arange(Q_BLOCK)
        sc = index_scores(qib, wib, ki)
        sc = jnp.where((spos[None, :] <= tpos[:, None])[None], sc, -jnp.inf)
        _, idx = lax.top_k(sc, topk)
        valid = idx <= tpos[None, :, None]
        return sparse_attend(qb, k[bi, idx], v[bi, idx], valid)

    out = lax.map(blk, jnp.arange(S // Q_BLOCK))
    return out.transpose(1, 0, 2, 3).reshape(B, S, D_B)


def dsa_sample(q, k_new, v_new, qi, ki_new, wi, ck, cv, cki, page_table):
    DB, T = q.shape[:2]
    L = PAST_LEN + T
    topk = min(TOPK_MAX, L // 4)
    ki_past = cki[page_table].reshape(DB, PAST_LEN, IDX_DIM)
    ki_all = jnp.concatenate([ki_past, ki_new], axis=1)
    sc = index_scores(qi, wi, ki_all)
    tpos = PAST_LEN + jnp.arange(T)
    spos = jnp.arange(L)
    sc = jnp.where((spos[None, :] <= tpos[:, None])[None], sc, -jnp.inf)
    _, idx = lax.top_k(sc, topk)
    valid = idx <= tpos[None, :, None]
    is_new = (idx >= PAST_LEN)[..., None, None]
    bi = jnp.arange(DB)[:, None, None]
    past_idx = jnp.minimum(idx, PAST_LEN - 1)
    phys = page_table[bi, past_idx // PAGE_SIZE]
    off = past_idx % PAGE_SIZE
    new_idx = jnp.clip(idx - PAST_LEN, 0, T - 1)
    k_sel = jnp.where(is_new, k_new[bi, new_idx], ck[phys, off])
    v_sel = jnp.where(is_new, v_new[bi, new_idx], cv[phys, off])
    return sparse_attend(q, k_sel, v_sel, valid)


def mem_kv(mem, g, wkv):
    B = mem.shape[0]
    k, v = jnp.split(rmsnorm(mem, g) @ wkv, 2, axis=-1)
    return (k.reshape(B, N_MEM, MEM_HEADS, MEM_HEAD_DIM), v.reshape(B, N_MEM, MEM_HEADS, MEM_HEAD_DIM))


def mem_attend(h, mk, mv, wq, wo):
    B, T = h.shape[:2]
    q = (h @ wq).reshape(B, T, MEM_HEADS, MEM_HEAD_DIM)
    s = jnp.einsum('bthd,bmhd->bhtm', q, mk, preferred_element_type=jnp.float32) * (MEM_HEAD_DIM ** -0.5)
    p = jax.nn.softmax(s, axis=-1)
    o = jnp.einsum('bhtm,bmhd->bthd', p.astype(mv.dtype), mv).reshape(B, T, D_MEM)
    return o @ wo


def peer_block(xb, wq, keys, u_tab, v_tab):
    T = xb.shape[0]
    q = (xb @ wq).reshape(T, PK_HEADS, 2, KEY_HALF)
    s = jnp.einsum('thcd,hcnd->thcn', q, keys, preferred_element_type=jnp.float32)
    sv, si = lax.top_k(s, PK_TOPK)
    cand = (sv[:, :, 0, :, None] + sv[:, :, 1, None, :]).reshape(T, PK_HEADS, PK_TOPK * PK_TOPK)
    cid = (si[:, :, 0, :, None] * N_KEYS + si[:, :, 1, None, :]).reshape(T, PK_HEADS, PK_TOPK * PK_TOPK)
    fv, fi = lax.top_k(cand, PK_TOPK)
    eid = jnp.take_along_axis(cid, fi, axis=-1)
    g = jax.nn.softmax(fv, axis=-1)
    a = jax.nn.gelu(jnp.einsum('thkd,td->thk', u_tab[eid], xb))
    return jnp.einsum('thk,thkd->td', (g * a).astype(xb.dtype), v_tab[eid])


def peer_apply(h, blk, wq, keys, u_tab, v_tab):
    Bt, T, D = h.shape
    hb = h.reshape(Bt * T // blk, blk, D)
    out = lax.map(lambda xb: peer_block(xb, wq, keys, u_tab, v_tab), hb)
    return out.reshape(Bt, T, D)


def setup_inputs(seed: int = 0) -> dict:
    key = jax.random.key(seed)
    ks = jax.random.split(key, 32)
    n_pages = PAST_LEN // PAGE_SIZE
    n_pool = (DEC_BATCH * n_pages * 5) // 4
    nrm = jax.random.normal
    f32 = jnp.float32
    page_table = jax.random.permutation(ks[0], n_pool)[:DEC_BATCH * n_pages].reshape(DEC_BATCH, n_pages).astype(jnp.int32)
    return {
        'x_prompt': nrm(ks[1], (BATCH, SEQ, D_MODEL), f32),
        'x_sample': nrm(ks[2], (DEC_BATCH, DEC_SEQ, D_MODEL), f32),
        'cache_k': nrm(ks[3], (DEPTH, n_pool, PAGE_SIZE, N_KV_HEADS, HEAD_DIM), f32),
        'cache_v': nrm(ks[4], (DEPTH, n_pool, PAGE_SIZE, N_KV_HEADS, HEAD_DIM), f32),
        'cache_kidx': nrm(ks[5], (DEPTH, n_pool, PAGE_SIZE, IDX_DIM), f32),
        'cache_mem_k': nrm(ks[6], (DEPTH, DEC_BATCH, N_MEM, MEM_HEADS, MEM_HEAD_DIM), f32),
        'cache_mem_v': nrm(ks[7], (DEPTH, DEC_BATCH, N_MEM, MEM_HEADS, MEM_HEAD_DIM), f32),
        'page_table': page_table,
        'mem_prompt': nrm(ks[8], (BATCH, N_MEM, D_MODEL), f32),
        'norm_mix_g': 1.0 + 0.01 * nrm(ks[9], (DEPTH, D_MODEL), f32),
        'w_in': nrm(ks[10], (DEPTH, D_MODEL, P_IN), f32) * D_MODEL ** -0.5,
        'gm_ln_g': 1.0 + 0.01 * nrm(ks[11], (DEPTH, D_A), f32),
        'gm_ln_b': 0.01 * nrm(ks[12], (DEPTH, D_A), f32),
        'gm_ws': nrm(ks[13], (DEPTH, GM_GROUPS, CHUNK, CHUNK), f32) * CHUNK ** -0.5,
        'gm_bs': 1.0 + 0.01 * nrm(ks[14], (DEPTH, GM_GROUPS, CHUNK), f32),
        'w_out': nrm(ks[15], (DEPTH, D_MIX, D_MODEL), f32) * D_MIX ** -0.5,
        'norm_mem_g': 1.0 + 0.01 * nrm(ks[16], (DEPTH, D_MODEL), f32),
        'mem_norm_g': 1.0 + 0.01 * nrm(ks[17], (DEPTH, D_MODEL), f32),
        'mem_wq': nrm(ks[18], (DEPTH, D_MODEL, D_MEM), f32) * D_MODEL ** -0.5,
        'mem_wkv': nrm(ks[19], (DEPTH, D_MODEL, 2 * D_MEM), f32) * D_MODEL ** -0.5,
        'mem_wo': nrm(ks[20], (DEPTH, D_MEM, D_MODEL), f32) * D_MEM ** -0.5,
        'norm_ffn_g': 1.0 + 0.01 * nrm(ks[21], (DEPTH, D_MODEL), f32),
        'peer_wq': nrm(ks[22], (DEPTH, D_MODEL, PK_HEADS * KEY_DIM), f32) * D_MODEL ** -0.5,
        'peer_keys': nrm(ks[23], (DEPTH, PK_HEADS, 2, N_KEYS, KEY_HALF), f32) * KEY_HALF ** -0.5,
        'peer_u': nrm(ks[24], (DEPTH, N_EXPERTS, D_MODEL), f32) * D_MODEL ** -0.5,
        'peer_v': nrm(ks[25], (DEPTH, N_EXPERTS, D_MODEL), f32) * 0.3,
        'final_norm_g': 1.0 + 0.01 * nrm(ks[26], (D_MODEL,), f32),
    }


def reference(x_prompt, x_sample, cache_k, cache_v, cache_kidx, cache_mem_k, cache_mem_v, page_table, mem_prompt,
              norm_mix_g, w_in, gm_ln_g, gm_ln_b, gm_ws, gm_bs, w_out,
              norm_mem_g, mem_norm_g, mem_wq, mem_wkv, mem_wo,
              norm_ffn_g, peer_wq, peer_keys, peer_u, peer_v, final_norm_g):
    xp, xs = x_prompt, x_sample
    S = x_prompt.shape[1]
    T = x_sample.shape[1]
    kp_l, vp_l, kip_l, gvp_l, mkp_l, mvp_l = [], [], [], [], [], []
    ks_l, vs_l, kis_l, gvs_l = [], [], [], []
    for l in range(DEPTH):
        ua, va, q, k, v, qi, ki, wi = split_proj(rmsnorm(xp, norm_mix_g[l]) @ w_in[l])
        ya, vn_p = gmlp_mix(ua, va, gm_ln_g[l], gm_ln_b[l], gm_ws[l], gm_bs[l], CHUNK)
        yb = dsa_prompt(q, k, v, qi, ki, wi)
        xp = xp + jnp.concatenate([ya, yb], axis=-1) @ w_out[l]
        kp_l.append(k); vp_l.append(v); kip_l.append(ki); gvp_l.append(vn_p[:, S - CHUNK:])

        ua_s, va_s, q_s, k_s, v_s, qi_s, ki_s, wi_s = split_proj(rmsnorm(xs, norm_mix_g[l]) @ w_in[l])
        ya_s, vn_s = gmlp_mix(ua_s, va_s, gm_ln_g[l], gm_ln_b[l], gm_ws[l], gm_bs[l], T)
        yb_s = dsa_sample(q_s, k_s, v_s, qi_s, ki_s, wi_s, cache_k[l], cache_v[l], cache_kidx[l], page_table)
        xs = xs + jnp.concatenate([ya_s, yb_s], axis=-1) @ w_out[l]
        ks_l.append(k_s); vs_l.append(v_s); kis_l.append(ki_s); gvs_l.append(vn_s)

        mk, mv = mem_kv(mem_prompt, mem_norm_g[l], mem_wkv[l])
        xp = xp + mem_attend(rmsnorm(xp, norm_mem_g[l]), mk, mv, mem_wq[l], mem_wo[l])
        xs = xs + mem_attend(rmsnorm(xs, norm_mem_g[l]), cache_mem_k[l], cache_mem_v[l], mem_wq[l], mem_wo[l])
        mkp_l.append(mk); mvp_l.append(mv)

        xp = xp + peer_apply(rmsnorm(xp, norm_ffn_g[l]), PEER_BLOCK, peer_wq[l], peer_keys[l], peer_u[l], peer_v[l])
        xs = xs + peer_apply(rmsnorm(xs, norm_ffn_g[l]), T, peer_wq[l], peer_keys[l], peer_u[l], peer_v[l])

    y_prompt = rmsnorm(xp, final_norm_g)
    y_sample = rmsnorm(xs, final_norm_g)
    k_prompt = jnp.stack(kp_l); v_prompt = jnp.stack(vp_l); kidx_prompt = jnp.stack(kip_l)
    gmv_prompt = jnp.stack(gvp_l); memk_prompt = jnp.stack(mkp_l); memv_prompt = jnp.stack(mvp_l)
    k_sample = jnp.stack(ks_l); v_sample = jnp.stack(vs_l); kidx_sample = jnp.stack(kis_l)
    gmv_sample = jnp.stack(gvs_l)
    return (y_prompt, y_sample, k_prompt, v_prompt, kidx_prompt, gmv_prompt, memk_prompt, memv_prompt,
            k_sample, v_sample, kidx_sample, gmv_sample)
```

```python
import functools

import jax
import jax.numpy as jnp
from jax import lax
from jax.experimental import pallas as pl
from jax.experimental.pallas import tpu as pltpu

F32 = jnp.float32
BF16 = jnp.bfloat16
I32 = jnp.int32

D_MODEL = 1024
D_A = 512
CHUNK = 128
GM_GROUPS = 4
HEAD_DIM = 64
N_HEADS = 8
N_KV_HEADS = 2
KV_W = 128
IDX_HEADS = 8
IDX_DIM = 64
TOPK_MAX = 256
Q_BLOCK = 128
PAGE_SIZE = 128
N_MEM = 256
MEM_HEADS = 4
MEM_HEAD_DIM = 128
D_MEM = 512
PK_HEADS = 8
N_KEYS = 128
KEY_HALF = 64
PK_TOPK = 16
EPS = 1e-6

LANES = 128
VMEM_LIMIT = 48 * 1024 * 1024

INT_MIN = -(2 ** 31)
NEG_INF = float("-inf")


def _cparams(sem):
    return pltpu.CompilerParams(dimension_semantics=sem, vmem_limit_bytes=VMEM_LIMIT)


def _sort_key(x):
    bits = pltpu.bitcast(x + 0.0, I32)
    return bits ^ ((bits >> 31) & 0x7FFFFFFF)


NEG_INF_KEY = INT_MIN + 0x7FFFFF


def _rms(x, g):
    ms = jnp.mean(x * x, axis=-1, keepdims=True)
    return x * lax.rsqrt(ms + EPS) * g


def _norm_proj_kernel(n_out, widths, x_ref, g_ref, *refs):
    w_refs, o_refs = refs[:n_out], refs[n_out:]
    h = _rms(x_ref[...], g_ref[...]).astype(BF16)
    for w_ref, o_ref, wd in zip(w_refs, o_refs, widths):
        z = jnp.dot(h, w_ref[...], preferred_element_type=F32)
        o_ref[...] = z[:, :wd].astype(o_ref.dtype)


def _norm_proj(x, g, weights, widths, dtypes, tm):
    rows, d = x.shape
    n = len(weights)
    in_specs = [pl.BlockSpec((tm, d), lambda i: (i, 0)), pl.BlockSpec((1, d), lambda i: (0, 0))]
    in_specs += [pl.BlockSpec(w.shape, lambda i: (0, 0)) for w in weights]
    out_specs = [pl.BlockSpec((tm, wd), lambda i: (i, 0)) for wd in widths]
    out_shape = [jax.ShapeDtypeStruct((rows, wd), dt) for wd, dt in zip(widths, dtypes)]
    return pl.pallas_call(
        functools.partial(_norm_proj_kernel, n, tuple(widths)),
        grid=(rows // tm,), in_specs=in_specs, out_specs=out_specs, out_shape=out_shape,
        compiler_params=_cparams(("parallel",)), name="norm_proj",
    )(x, g, *weights)


def _gmlp_kernel(n_mix, c_rows, ua_ref, va_ref, lng_ref, lnb_ref, wmix_ref, bias_ref, ya_ref, vn_ref):
    u = jax.nn.gelu(ua_ref[...])
    gv = jax.nn.gelu(va_ref[...])
    mu = jnp.mean(gv, axis=-1, keepdims=True)
    xc = gv - mu
    var = jnp.mean(xc * xc, axis=-1, keepdims=True)
    vn = xc * lax.rsqrt(var + EPS) * lng_ref[...] + lnb_ref[...]
    vn_ref[...] = vn
    vnb = vn.astype(BF16)
    for c in range(n_mix):
        r0 = c * c_rows
        for g in range(GM_GROUPS):
            l0 = g * LANES
            mixed = jnp.dot(wmix_ref[g], vnb[r0:r0 + c_rows, l0:l0 + LANES], preferred_element_type=F32)
            mixed = mixed + bias_ref[:, l0:l0 + LANES]
            ya_ref[r0:r0 + c_rows, l0:l0 + LANES] = (u[r0:r0 + c_rows, l0:l0 + LANES] * mixed).astype(ya_ref.dtype)


def _gmlp(ua, va, ln_g, ln_b, wmix, bias, tm):
    rows = ua.shape[0]
    c_rows = wmix.shape[1]
    n_mix = tm // c_rows
    return pl.pallas_call(
        functools.partial(_gmlp_kernel, n_mix, c_rows),
        grid=(rows // tm,),
        in_specs=[pl.BlockSpec((tm, D_A), lambda i: (i, 0)), pl.BlockSpec((tm, D_A), lambda i: (i, 0)),
                  pl.BlockSpec((1, D_A), lambda i: (0, 0)), pl.BlockSpec((1, D_A), lambda i: (0, 0)),
                  pl.BlockSpec(wmix.shape, lambda i: (0, 0, 0)), pl.BlockSpec(bias.shape, lambda i: (0, 0))],
        out_specs=[pl.BlockSpec((tm, D_A), lambda i: (i, 0)), pl.BlockSpec((tm, D_A), lambda i: (i, 0))],
        out_shape=[jax.ShapeDtypeStruct((rows, D_A), BF16), jax.ShapeDtypeStruct((rows, D_A), F32)],
        compiler_params=_cparams(("parallel",)), name="gmlp",
    )(ua, va, ln_g, ln_b, wmix, bias)


def _dsa_prompt_kernel(topk, qi_ref, w_ref, q2_ref, ki_ref, k_ref, vt_ref, o_ref, key_scr):
    blk = pl.program_id(1)
    nk = blk + 1
    qpos = blk * Q_BLOCK + lax.broadcasted_iota(I32, (Q_BLOCK, Q_BLOCK), 1)
    row = lax.broadcasted_iota(I32, (Q_BLOCK, Q_BLOCK), 0)
    w_row = w_ref[0, 0] * (IDX_HEADS ** -0.5)
    qi = qi_ref[0, 0]

    def score_chunk(c, carry):
        s = lax.dot_general(ki_ref[0, c], qi, (((1,), (1,)), ((), ())), preferred_element_type=F32)
        s = jnp.maximum(s, 0.0) * w_row
        sc = s[:, 0:LANES]
        for h in range(1, IDX_HEADS):
            sc = sc + s[:, h * LANES:(h + 1) * LANES]
        kpos = c * Q_BLOCK + row
        key_scr[c] = jnp.where(kpos <= qpos, _sort_key(sc), NEG_INF_KEY)
        return carry

    lax.fori_loop(0, nk, score_chunk, 0)

    def count(pred):
        def body(c, acc):
            return acc + pred(key_scr[c], c).astype(I32)
        acc = lax.fori_loop(0, nk, body, jnp.zeros((Q_BLOCK, Q_BLOCK), I32))
        return jnp.sum(acc, axis=0, keepdims=True)

    lo = jnp.where(count(lambda kk, c: kk >= 0) >= topk, 0, INT_MIN).astype(I32)

    def bit_step(it, lo):
        cand = lo + jnp.left_shift(jnp.int32(1), 30 - it)
        return jnp.where(count(lambda kk, c: kk >= cand) >= topk, cand, lo)

    theta = lax.fori_loop(0, 31, bit_step, lo)
    need = topk - count(lambda kk, c: kk > theta)

    def pos_step(it, ans):
        cand = ans + jnp.left_shift(jnp.int32(1), 10 - it)
        below = count(lambda kk, c: (kk == theta) & (c * Q_BLOCK + row < cand))
        return jnp.where(below < need, cand, ans)

    pcut = lax.fori_loop(0, 11, pos_step, jnp.zeros((1, Q_BLOCK), I32))

    q2 = q2_ref[0, 0]
    n_hq = N_HEADS * Q_BLOCK

    def attn_chunk(c, carry):
        m, l, acc = carry
        kk = key_scr[c]
        kpos = c * Q_BLOCK + row
        sel = ((kk > theta) | ((kk == theta) & (kpos <= pcut))) & (kk > NEG_INF_KEY)
        bias = jnp.where(sel, 0.0, NEG_INF)
        s = lax.dot_general(k_ref[0, c], q2, (((1,), (1,)), ((), ())), preferred_element_type=F32)
        s = s * (HEAD_DIM ** -0.5) + jnp.concatenate([bias] * N_HEADS, axis=1)
        m_new = jnp.maximum(m, jnp.max(s, axis=0, keepdims=True))
        m_safe = jnp.where(m_new == NEG_INF, 0.0, m_new)
        alpha = jnp.exp(m - m_safe)
        p = jnp.exp(s - m_safe)
        l = alpha * l + jnp.sum(p, axis=0, keepdims=True)
        acc = alpha * acc + jnp.dot(vt_ref[0, c], p.astype(BF16), preferred_element_type=F32)
        return m_new, l, acc

    m0 = jnp.full((1, n_hq), NEG_INF, F32)
    l0 = jnp.zeros((1, n_hq), F32)
    a0 = jnp.zeros((KV_W, n_hq), F32)
    _, l, acc = lax.fori_loop(0, nk, attn_chunk, (m0, l0, a0))
    o = acc / l
    half = n_hq // 2
    o_ref[0, 0] = jnp.concatenate([o[0:HEAD_DIM, 0:half], o[HEAD_DIM:KV_W, half:n_hq]], axis=1).astype(o_ref.dtype)


def _dsa_prompt(qi2, w_row, q2, ki4, k4, vt4, topk):
    b, nblk = qi2.shape[0], qi2.shape[1]
    nch = ki4.shape[1]
    n_hq = N_HEADS * Q_BLOCK
    return pl.pallas_call(
        functools.partial(_dsa_prompt_kernel, topk),
        grid=(b, nblk),
        in_specs=[pl.BlockSpec((1, 1, n_hq, IDX_DIM), lambda i, j: (i, j, 0, 0)),
                  pl.BlockSpec((1, 1, 1, n_hq), lambda i, j: (i, j, 0, 0)),
                  pl.BlockSpec((1, 1, n_hq, KV_W), lambda i, j: (i, j, 0, 0)),
                  pl.BlockSpec((1, nch, Q_BLOCK, IDX_DIM), lambda i, j: (i, 0, 0, 0)),
                  pl.BlockSpec((1, nch, Q_BLOCK, KV_W), lambda i, j: (i, 0, 0, 0)),
                  pl.BlockSpec((1, nch, KV_W, Q_BLOCK), lambda i, j: (i, 0, 0, 0))],
        out_specs=pl.BlockSpec((1, 1, HEAD_DIM, n_hq), lambda i, j: (i, j, 0, 0)),
        out_shape=jax.ShapeDtypeStruct((b, nblk, HEAD_DIM, n_hq), BF16),
        scratch_shapes=[pltpu.VMEM((nch, Q_BLOCK, Q_BLOCK), I32)],
        compiler_params=_cparams(("parallel", "arbitrary")), name="dsa_prompt",
    )(qi2, w_row, q2, ki4, k4, vt4)


def _dsa_sample_kernel(topk, n_pages, t_q, pt_ref, qi_ref, w_ref, q2_ref, kin_ref, kn_ref, vn_ref,
                       cki_hbm, ck_hbm, cv_hbm, o_ref, ki_buf, k_buf, v_buf, key_scr, s_scr, sem):
    b = pl.program_id(0)
    npg = n_pages + 1

    def page_copies(p):
        pid = pt_ref[b, p]
        return (pltpu.make_async_copy(cki_hbm.at[pid], ki_buf.at[p], sem.at[0]),
                pltpu.make_async_copy(ck_hbm.at[pid], k_buf.at[p], sem.at[1]),
                pltpu.make_async_copy(cv_hbm.at[pid], v_buf.at[p], sem.at[2]))

    def start(p, c):
        for cp in page_copies(p):
            cp.start()
        return c

    lax.fori_loop(0, n_pages, start, 0)
    ki_buf[n_pages] = kin_ref[0]
    k_buf[n_pages] = kn_ref[0]
    v_buf[n_pages] = vn_ref[0]

    def wait(p, c):
        for cp in page_copies(p):
            cp.wait()
        return c

    lax.fori_loop(0, n_pages, wait, 0)

    n_rows = t_q * IDX_HEADS
    qi = qi_ref[0]
    w_col = w_ref[0] * (IDX_HEADS ** -0.5)
    lane = lax.broadcasted_iota(I32, (t_q, LANES), 1)
    qrow = lax.broadcasted_iota(I32, (t_q, LANES), 0)

    def score_page(p, carry):
        s = lax.dot_general(qi, ki_buf[p].astype(BF16), (((1,), (1,)), ((), ())), preferred_element_type=F32)
        s = jnp.maximum(s, 0.0) * w_col
        sc = jnp.sum(s.reshape(t_q, IDX_HEADS, LANES), axis=1)
        key = _sort_key(sc)
        key = jnp.where((p < n_pages) | (lane <= qrow), key, NEG_INF_KEY)
        key_scr[p] = key
        return carry

    lax.fori_loop(0, npg, score_page, 0)

    def count(pred):
        def body(p, acc):
            return acc + pred(key_scr[p], p).astype(I32)
        acc = lax.fori_loop(0, npg, body, jnp.zeros((t_q, LANES), I32))
        return jnp.sum(acc, axis=1, keepdims=True)

    lo = jnp.where(count(lambda kk, p: kk >= 0) >= topk, 0, INT_MIN).astype(I32)

    def bit_step(it, lo):
        cand = lo + jnp.left_shift(jnp.int32(1), 30 - it)
        return jnp.where(count(lambda kk, p: kk >= cand) >= topk, cand, lo)

    theta = lax.fori_loop(0, 31, bit_step, lo)
    need = topk - count(lambda kk, p: kk > theta)

    def pos_step(it, ans):
        cand = ans + jnp.left_shift(jnp.int32(1), 13 - it)
        below = count(lambda kk, p: (kk == theta) & (p * PAGE_SIZE + lane < cand))
        return jnp.where(below < need, cand, ans)

    pcut = lax.fori_loop(0, 14, pos_step, jnp.zeros((t_q, 1), I32))

    q2 = q2_ref[0]

    def logits_page(p, mx):
        kk = key_scr[p]
        sel = ((kk > theta) | ((kk == theta) & (p * PAGE_SIZE + lane <= pcut))) & (kk > NEG_INF_KEY)
        bias = jnp.where(sel, 0.0, NEG_INF)
        bias = jnp.broadcast_to(bias[:, None, :], (t_q, N_HEADS, LANES)).reshape(n_rows, LANES)
        s = lax.dot_general(q2, k_buf[p].astype(BF16), (((1,), (1,)), ((), ())), preferred_element_type=F32)
        s = s * (HEAD_DIM ** -0.5) + bias
        s_scr[p] = s
        return jnp.maximum(mx, s)

    mx = lax.fori_loop(0, npg, logits_page, jnp.full((n_rows, LANES), NEG_INF, F32))
    m = jnp.max(mx, axis=1, keepdims=True)

    def pv_page(p, carry):
        l, acc = carry
        pr = jnp.exp(s_scr[p] - m)
        acc = acc + jnp.dot(pr.astype(BF16), v_buf[p].astype(BF16), preferred_element_type=F32)
        return l + pr, acc

    l, acc = lax.fori_loop(0, npg, pv_page, (jnp.zeros((n_rows, LANES), F32), jnp.zeros((n_rows, KV_W), F32)))
    o = acc / jnp.sum(l, axis=1, keepdims=True)
    head = lax.broadcasted_iota(I32, (n_rows, HEAD_DIM), 0) % N_HEADS
    o_ref[0] = jnp.where(head < N_HEADS // N_KV_HEADS, o[:, 0:HEAD_DIM], o[:, HEAD_DIM:KV_W]).astype(o_ref.dtype)


def _dsa_sample(page_table, qi_s, w_s, q2_s, ki_new, k_new, v_new, cki, ck, cv, topk):
    db, n_pages = page_table.shape
    n_rows = qi_s.shape[1]
    t_q = n_rows // IDX_HEADS
    npg = n_pages + 1
    grid_spec = pltpu.PrefetchScalarGridSpec(
        num_scalar_prefetch=1, grid=(db,),
        in_specs=[pl.BlockSpec((1, n_rows, IDX_DIM), lambda i, pt: (i, 0, 0)),
                  pl.BlockSpec((1, n_rows, 1), lambda i, pt: (i, 0, 0)),
                  pl.BlockSpec((1, n_rows, KV_W), lambda i, pt: (i, 0, 0)),
                  pl.BlockSpec((1, PAGE_SIZE, IDX_DIM), lambda i, pt: (i, 0, 0)),
                  pl.BlockSpec((1, PAGE_SIZE, KV_W), lambda i, pt: (i, 0, 0)),
                  pl.BlockSpec((1, PAGE_SIZE, KV_W), lambda i, pt: (i, 0, 0)),
                  pl.BlockSpec(memory_space=pl.ANY), pl.BlockSpec(memory_space=pl.ANY),
                  pl.BlockSpec(memory_space=pl.ANY)],
        out_specs=pl.BlockSpec((1, n_rows, HEAD_DIM), lambda i, pt: (i, 0, 0)),
        scratch_shapes=[pltpu.VMEM((npg, PAGE_SIZE, IDX_DIM), F32),
                        pltpu.VMEM((npg, PAGE_SIZE, KV_W), F32),
                        pltpu.VMEM((npg, PAGE_SIZE, KV_W), F32),
                        pltpu.VMEM((npg, t_q, LANES), I32),
                        pltpu.VMEM((npg, n_rows, LANES), F32),
                        pltpu.SemaphoreType.DMA((3,))])
    return pl.pallas_call(
        functools.partial(_dsa_sample_kernel, topk, n_pages, t_q),
        grid_spec=grid_spec,
        out_shape=jax.ShapeDtypeStruct((db, n_rows, HEAD_DIM), BF16),
        compiler_params=_cparams(("arbitrary",)), name="dsa_sample",
    )(page_table, qi_s, w_s, q2_s, ki_new, k_new, v_new, cki, ck, cv)


def _mem_attend_rows(q, mk, mv):
    outs = []
    for h in range(MEM_HEADS):
        sl = slice(h * MEM_HEAD_DIM, (h + 1) * MEM_HEAD_DIM)
        s = lax.dot_general(q[:, sl], mk[:, sl], (((1,), (1,)), ((), ())), preferred_element_type=F32)
        s = s * (MEM_HEAD_DIM ** -0.5)
        p = jnp.exp(s - jnp.max(s, axis=-1, keepdims=True))
        p = p / jnp.sum(p, axis=-1, keepdims=True)
        outs.append(jnp.dot(p.astype(BF16), mv[:, sl], preferred_element_type=F32))
    return jnp.concatenate(outs, axis=1)


def _post_mix_kernel(n_grp, rows_per, x_ref, ya_ref, yb_ref, woa_ref, wob_ref, gmem_ref, wq_ref, mk_ref, mv_ref,
                     wo_ref, gffn_ref, x2_ref, h3_ref):
    x1 = x_ref[...] + jnp.dot(ya_ref[...], woa_ref[...], preferred_element_type=F32)
    x1 = x1 + jnp.dot(yb_ref[...], wob_ref[...], preferred_element_type=F32)
    h = _rms(x1, gmem_ref[...]).astype(BF16)
    q = jnp.dot(h, wq_ref[...], preferred_element_type=F32).astype(BF16)
    if n_grp == 1:
        o = _mem_attend_rows(q, mk_ref[0].astype(BF16), mv_ref[0].astype(BF16))
    else:
        o = jnp.concatenate(
            [_mem_attend_rows(q[g * rows_per:(g + 1) * rows_per], mk_ref[g].astype(BF16), mv_ref[g].astype(BF16))
             for g in range(n_grp)], axis=0)
    x2 = x1 + jnp.dot(o.astype(BF16), wo_ref[...], preferred_element_type=F32)
    x2_ref[...] = x2
    h3_ref[...] = _rms(x2, gffn_ref[...]).astype(BF16)


def _post_mix(x, ya, yb, woa, wob, gmem, wq, mk, mv, wo, gffn, tm, rows_per_mem):
    rows = x.shape[0]
    if rows_per_mem >= tm:
        n_grp, rows_per = 1, tm
        per = rows_per_mem // tm
        mem_map = lambda i: (i // per, 0, 0)
    else:
        n_grp, rows_per = tm // rows_per_mem, rows_per_mem
        mem_map = lambda i: (i, 0, 0)
    full = lambda a: pl.BlockSpec(a.shape, lambda i: (0,) * a.ndim)
    return pl.pallas_call(
        functools.partial(_post_mix_kernel, n_grp, rows_per),
        grid=(rows // tm,),
        in_specs=[pl.BlockSpec((tm, D_MODEL), lambda i: (i, 0)), pl.BlockSpec((tm, D_A), lambda i: (i, 0)),
                  pl.BlockSpec((tm, D_A), lambda i: (i, 0)), full(woa), full(wob), full(gmem), full(wq),
                  pl.BlockSpec((n_grp, N_MEM, D_MEM), mem_map), pl.BlockSpec((n_grp, N_MEM, D_MEM), mem_map),
                  full(wo), full(gffn)],
        out_specs=[pl.BlockSpec((tm, D_MODEL), lambda i: (i, 0)), pl.BlockSpec((tm, D_MODEL), lambda i: (i, 0))],
        out_shape=[jax.ShapeDtypeStruct((rows, D_MODEL), F32), jax.ShapeDtypeStruct((rows, D_MODEL), BF16)],
        compiler_params=_cparams(("parallel",)), name="post_mix",
    )(x, ya, yb, woa, wob, gmem, wq, mk, mv, wo, gffn)


def _top_values(s, n):
    vals = []
    for _ in range(n):
        m = jnp.max(s, axis=0, keepdims=True)
        vals.append(m)
        s = jnp.where(s == m, NEG_INF, s)
    return vals


def _peer_gate_kernel(h_ref, wq_ref, keys_ref, s0_ref, s1_ref, e0_ref, e1_ref, th_ref):
    q = jnp.dot(h_ref[...], wq_ref[...], preferred_element_type=F32).astype(BF16)
    st = lax.dot_general(keys_ref[...], q, (((1,), (1,)), ((), ())), preferred_element_type=F32)
    thetas = []
    for h in range(PK_HEADS):
        s0 = st[(2 * h) * N_KEYS:(2 * h + 1) * N_KEYS]
        s1 = st[(2 * h + 1) * N_KEYS:(2 * h + 2) * N_KEYS]
        a = _top_values(s0, PK_TOPK)
        b = _top_values(s1, PK_TOPK)
        cand = [a[k] + b[l] for k in range(PK_TOPK) for l in range(PK_TOPK) if (k + 1) * (l + 1) <= PK_TOPK]
        top = _top_values(jnp.concatenate(cand, axis=0), PK_TOPK)
        z = top[0] - top[0] + 1.0
        for r in range(1, PK_TOPK):
            z = z + jnp.exp(top[r] - top[0])
        s0_ref[h] = s0
        s1_ref[h] = s1
        e0_ref[h] = jnp.exp(s0 - a[0]) / z
        e1_ref[h] = jnp.exp(s1 - b[0])
        thetas.append(top[PK_TOPK - 1])
    th_ref[...] = jnp.concatenate(thetas, axis=0)


def _peer_gate(h3, wq, keys_pad, tt):
    rows = h3.shape[0]
    big = jax.ShapeDtypeStruct((PK_HEADS, N_KEYS, rows), F32)
    bspec = pl.BlockSpec((PK_HEADS, N_KEYS, tt), lambda i: (0, 0, i))
    return pl.pallas_call(
        _peer_gate_kernel,
        grid=(rows // tt,),
        in_specs=[pl.BlockSpec((tt, D_MODEL), lambda i: (i, 0)), pl.BlockSpec(wq.shape, lambda i: (0, 0)),
                  pl.BlockSpec(keys_pad.shape, lambda i: (0, 0))],
        out_specs=[bspec, bspec, bspec, bspec, pl.BlockSpec((PK_HEADS, tt), lambda i: (0, i))],
        out_shape=[big, big, big, big, jax.ShapeDtypeStruct((PK_HEADS, rows), F32)],
        compiler_params=_cparams(("parallel",)), name="peer_gate",
    )(h3, wq, keys_pad)


def _peer_dense_kernel(n_i, h_ref, x_ref, u_ref, v_ref, s0_ref, s1_ref, e0_ref, e1_ref, th_ref, gfin_ref,
                       y_ref, acc_ref):
    e = pl.program_id(1)

    @pl.when(e == 0)
    def _():
        acc_ref[...] = jnp.zeros_like(acc_ref)

    at = lax.dot_general(u_ref[...], h_ref[...], (((1,), (1,)), ((), ())), preferred_element_type=F32)
    at = jax.nn.gelu(at)
    ws = []
    for ii in range(n_i):
        i = e * n_i + ii
        g = None
        for h in range(PK_HEADS):
            ssum = s0_ref[h, pl.ds(i, 1), :] + s1_ref[h]
            val = e0_ref[h, pl.ds(i, 1), :] * e1_ref[h]
            gh = jnp.where(ssum >= th_ref[pl.ds(h, 1), :], val, 0.0)
            g = gh if g is None else g + gh
        ws.append((at[ii * N_KEYS:(ii + 1) * N_KEYS] * g).astype(BF16))
    wt = jnp.concatenate(ws, axis=0) if n_i > 1 else ws[0]
    acc_ref[...] += lax.dot_general(wt, v_ref[...], (((0,), (0,)), ((), ())), preferred_element_type=F32)

    @pl.when(e == pl.num_programs(1) - 1)
    def _():
        y_ref[...] = _rms(x_ref[...] + acc_ref[...], gfin_ref[...])


def _peer_dense(h3, x2, u, v, s0, s1, e0, e1, th, gfin, tt, eb):
    rows = h3.shape[0]
    n_exp = u.shape[0]
    n_i = eb // N_KEYS
    gspec = pl.BlockSpec((PK_HEADS, N_KEYS, tt), lambda t, e: (0, 0, t))
    return pl.pallas_call(
        functools.partial(_peer_dense_kernel, n_i),
        grid=(rows // tt, n_exp // eb),
        in_specs=[pl.BlockSpec((tt, D_MODEL), lambda t, e: (t, 0)), pl.BlockSpec((tt, D_MODEL), lambda t, e: (t, 0)),
                  pl.BlockSpec((eb, D_MODEL), lambda t, e: (e, 0)), pl.BlockSpec((eb, D_MODEL), lambda t, e: (e, 0)),
                  gspec, gspec, gspec, gspec, pl.BlockSpec((PK_HEADS, tt), lambda t, e: (0, t)),
                  pl.BlockSpec((1, D_MODEL), lambda t, e: (0, 0))],
        out_specs=pl.BlockSpec((tt, D_MODEL), lambda t, e: (t, 0)),
        out_shape=jax.ShapeDtypeStruct((rows, D_MODEL), F32),
        scratch_shapes=[pltpu.VMEM((tt, D_MODEL), F32)],
        compiler_params=_cparams(("parallel", "arbitrary")), name="peer_dense",
    )(h3, x2, u, v, s0, s1, e0, e1, th, gfin)


def _heads_major(a, nblk, blk, heads, dim):
    b = a.shape[0] // (nblk * blk)
    return a.reshape(b, nblk, blk, heads, dim).transpose(0, 1, 3, 2, 4)


def _pad_kv_groups(q5):
    z = jnp.zeros_like(q5)
    half = N_HEADS // N_KV_HEADS
    lo = jnp.concatenate([q5[..., :half, :, :], z[..., :half, :, :]], axis=-1)
    hi = jnp.concatenate([z[..., half:, :, :], q5[..., half:, :, :]], axis=-1)
    return jnp.concatenate([lo, hi], axis=-3)


def kernel(x_prompt, x_sample, cache_k, cache_v, cache_kidx, cache_mem_k, cache_mem_v, page_table, mem_prompt,
           norm_mix_g, w_in, gm_ln_g, gm_ln_b, gm_ws, gm_bs, w_out, norm_mem_g, mem_norm_g, mem_wq, mem_wkv, mem_wo,
           norm_ffn_g, peer_wq, peer_keys, peer_u, peer_v, final_norm_g):
    assert w_in.shape[0] == 1, "one layer"
    B, S, D = x_prompt.shape
    DB, T, _ = x_sample.shape
    nblk = S // Q_BLOCK
    xp = x_prompt.reshape(B * S, D)
    xs = x_sample.reshape(DB * T, D)
    row = lambda a: a.reshape(1, -1)

    sizes = (D_A, D_A, N_HEADS * HEAD_DIM, KV_W, KV_W, IDX_HEADS * IDX_DIM, IDX_DIM, IDX_HEADS)
    offs = [0]
    for s_ in sizes:
        offs.append(offs[-1] + s_)
    w_parts = [w_in[0][:, offs[i]:offs[i + 1]].astype(BF16) for i in range(len(sizes))]
    w_parts[-1] = jnp.pad(w_parts[-1], ((0, 0), (0, LANES - IDX_HEADS)))
    dts = (F32, F32, BF16, F32, F32, BF16, F32, F32)
    g_mix = row(norm_mix_g[0])
    ua, va, q, k, v, qi, ki, wi = _norm_proj(xp, g_mix, w_parts, sizes, dts, tm=512)
    ua_s, va_s, q_s, k_s, v_s, qi_s, ki_s, wi_s = _norm_proj(xs, g_mix, w_parts, sizes, dts, tm=DB * T)

    tril = jnp.tril(jnp.ones((CHUNK, CHUNK), bool))
    wmix_p = jnp.where(tril, gm_ws[0], 0).astype(BF16)
    bias_p = jnp.repeat(gm_bs[0].T, LANES, axis=1)
    ya, vn_p = _gmlp(ua, va, row(gm_ln_g[0]), row(gm_ln_b[0]), wmix_p, bias_p, tm=512)
    w_small = jnp.where(tril[:T, :T], gm_ws[0][:, :T, :T], 0)
    wmix_s = jnp.einsum("ab,gts->gatbs", jnp.eye(DB, dtype=F32), w_small).reshape(GM_GROUPS, DB * T, DB * T)
    bias_s = jnp.tile(jnp.repeat(gm_bs[0][:, :T].T, LANES, axis=1), (DB, 1))
    ya_s, vn_s = _gmlp(ua_s, va_s, row(gm_ln_g[0]), row(gm_ln_b[0]), wmix_s.astype(BF16), bias_s, tm=DB * T)

    topk_p = min(TOPK_MAX, S // 4)
    n_hq = N_HEADS * Q_BLOCK
    qi2 = _heads_major(qi, nblk, Q_BLOCK, IDX_HEADS, IDX_DIM).reshape(B, nblk, n_hq, IDX_DIM)
    w_rowp = wi.reshape(B, nblk, Q_BLOCK, IDX_HEADS).transpose(0, 1, 3, 2).reshape(B, nblk, 1, n_hq)
    q2 = _pad_kv_groups(_heads_major(q, nblk, Q_BLOCK, N_HEADS, HEAD_DIM)).reshape(B, nblk, n_hq, KV_W)
    ki4 = ki.astype(BF16).reshape(B, nblk, Q_BLOCK, IDX_DIM)
    k4 = k.astype(BF16).reshape(B, nblk, Q_BLOCK, KV_W)
    vt4 = v.astype(BF16).reshape(B, nblk, Q_BLOCK, KV_W).transpose(0, 1, 3, 2)
    o_t = _dsa_prompt(qi2, w_rowp, q2, ki4, k4, vt4, topk_p)
    yb = o_t.reshape(B, nblk, HEAD_DIM, N_HEADS, Q_BLOCK).transpose(0, 1, 4, 3, 2).reshape(B * S, N_HEADS * HEAD_DIM)

    n_pages = page_table.shape[1]
    topk_s = min(TOPK_MAX, (n_pages * PAGE_SIZE + T) // 4)
    n_pool = cache_k.shape[1]
    qi_sr = qi_s.reshape(DB, T * IDX_HEADS, IDX_DIM)
    w_sr = wi_s.reshape(DB, T * IDX_HEADS, 1)
    q5_s = q_s.reshape(DB, T, N_HEADS, HEAD_DIM).transpose(0, 2, 1, 3)
    q2_s = _pad_kv_groups(q5_s).transpose(0, 2, 1, 3).reshape(DB, T * N_HEADS, KV_W)
    pad_page = lambda a: jnp.pad(a.reshape(DB, T, -1), ((0, 0), (0, PAGE_SIZE - T), (0, 0)))
    o_s = _dsa_sample(page_table, qi_sr, w_sr, q2_s, pad_page(ki_s), pad_page(k_s), pad_page(v_s),
                      cache_kidx[0], cache_k[0].reshape(n_pool, PAGE_SIZE, KV_W),
                      cache_v[0].reshape(n_pool, PAGE_SIZE, KV_W), topk_s)
    yb_s = o_s.reshape(DB * T, N_HEADS * HEAD_DIM)

    wkv = mem_wkv[0].astype(BF16)
    mk_p, mv_p = _norm_proj(mem_prompt.reshape(B * N_MEM, D), row(mem_norm_g[0]), [wkv[:, :D_MEM], wkv[:, D_MEM:]],
                            (D_MEM, D_MEM), (F32, F32), tm=512)

    woa, wob = w_out[0][:D_A].astype(BF16), w_out[0][D_A:].astype(BF16)
    post = functools.partial(_post_mix, woa=woa, wob=wob, gmem=row(norm_mem_g[0]), wq=mem_wq[0].astype(BF16),
                             wo=mem_wo[0].astype(BF16), gffn=row(norm_ffn_g[0]))
    x2_p, h3_p = post(xp, ya, yb, mk=mk_p.reshape(B, N_MEM, D_MEM), mv=mv_p.reshape(B, N_MEM, D_MEM),
                      tm=512, rows_per_mem=S)
    x2_s, h3_s = post(xs, ya_s, yb_s, mk=cache_mem_k[0].reshape(DB, N_MEM, D_MEM),
                      mv=cache_mem_v[0].reshape(DB, N_MEM, D_MEM), tm=32, rows_per_mem=T)

    x2 = jnp.concatenate([x2_p, x2_s], axis=0)
    h3 = jnp.concatenate([h3_p, h3_s], axis=0)
    keys = peer_keys[0].reshape(PK_HEADS * 2, N_KEYS, KEY_HALF)
    eye = jnp.eye(PK_HEADS * 2, dtype=F32)
    keys_pad = jnp.einsum("gnd,gf->gnfd", keys, eye).reshape(PK_HEADS * 2 * N_KEYS, PK_HEADS * 2 * KEY_HALF)
    s0, s1, e0, e1, th = _peer_gate(h3, peer_wq[0].astype(BF16), keys_pad.astype(BF16), tt=512)
    y = _peer_dense(h3, x2, peer_u[0].astype(BF16), peer_v[0].astype(BF16), s0, s1, e0, e1, th,
                    row(final_norm_g), tt=512, eb=512)

    y_prompt = y[:B * S].reshape(B, S, D)
    y_sample = y[B * S:].reshape(DB, T, D)
    shp = lambda a, *s: a.reshape(1, *s)
    return (y_prompt, y_sample,
            shp(k, B, S, N_KV_HEADS, HEAD_DIM), shp(v, B, S, N_KV_HEADS, HEAD_DIM), shp(ki, B, S, IDX_DIM),
            shp(vn_p.reshape(B, S, D_A)[:, S - CHUNK:], B, CHUNK, D_A),
            shp(mk_p, B, N_MEM, MEM_HEADS, MEM_HEAD_DIM), shp(mv_p, B, N_MEM, MEM_HEADS, MEM_HEAD_DIM),
            shp(k_s, DB, T, N_KV_HEADS, HEAD_DIM), shp(v_s, DB, T, N_KV_HEADS, HEAD_DIM), shp(ki_s, DB, T, IDX_DIM),
            shp(vn_s, DB, T, D_A))
```

```python
import functools

import jax
import jax.numpy as jnp
from jax import lax
from jax.experimental import pallas as pl
from jax.experimental.pallas import tpu as pltpu

F32 = jnp.float32
BF16 = jnp.bfloat16
I32 = jnp.int32

D_MODEL = 1024
D_A = 512
CHUNK = 128
GM_GROUPS = 4
HEAD_DIM = 64
N_HEADS = 8
N_KV_HEADS = 2
KV_W = 128
IDX_HEADS = 8
IDX_DIM = 64
TOPK_MAX = 256
Q_BLOCK = 128
PAGE_SIZE = 128
N_MEM = 256
MEM_HEADS = 4
MEM_HEAD_DIM = 128
D_MEM = 512
PK_HEADS = 8
N_KEYS = 128
KEY_HALF = 64
PK_TOPK = 16
EPS = 1e-6

LANES = 128
BF16_ROWS = 16
VMEM_LIMIT = 48 * 1024 * 1024
VMEM_LIMIT_PAGED = 56 * 1024 * 1024

INT_MIN = -(2 ** 31)
NEG_INF = float("-inf")
NEG_INF_KEY = INT_MIN + 0x7FFFFF


def _cparams(sem, vmem=VMEM_LIMIT):
    return pltpu.CompilerParams(dimension_semantics=sem, vmem_limit_bytes=vmem)


def _key_to_float(key):
    return pltpu.bitcast(key ^ ((key >> 31) & 0x7FFFFFFF), F32)


def _kth_largest(count_ge, topk, shape):
    lo = jnp.where(count_ge(jnp.zeros(shape, F32)) >= topk, 0, NEG_INF_KEY).astype(I32)

    def bit_step(it, lo):
        cand = lo + jnp.left_shift(jnp.int32(1), 30 - it)
        return jnp.where(count_ge(_key_to_float(cand)) >= topk, cand, lo)

    return _key_to_float(lax.fori_loop(0, 31, bit_step, lo))


def _tie_cutoff(count_eq_below, need, n_bits, shape):
    def pos_step(it, ans):
        cand = ans + jnp.left_shift(jnp.int32(1), n_bits - 1 - it)
        return jnp.where(count_eq_below(cand) < need, cand, ans)

    return lax.fori_loop(0, n_bits, pos_step, jnp.zeros(shape, I32))


def _rms(x, g):
    ms = jnp.mean(x * x, axis=-1, keepdims=True)
    return x * lax.rsqrt(ms + EPS) * g


def _norm_proj_kernel(n_out, widths, x_ref, g_ref, *refs):
    w_refs, o_refs = refs[:n_out], refs[n_out:]
    h = _rms(x_ref[...], g_ref[...]).astype(BF16)
    for w_ref, o_ref, wd in zip(w_refs, o_refs, widths):
        z = jnp.dot(h, w_ref[...], preferred_element_type=F32)
        o_ref[...] = z[:, :wd].astype(o_ref.dtype)


def _norm_proj(x, g, weights, widths, dtypes, tm):
    rows, d = x.shape
    n = len(weights)
    in_specs = [pl.BlockSpec((tm, d), lambda i: (i, 0)), pl.BlockSpec((1, d), lambda i: (0, 0))]
    in_specs += [pl.BlockSpec(w.shape, lambda i: (0, 0)) for w in weights]
    out_specs = [pl.BlockSpec((tm, wd), lambda i: (i, 0)) for wd in widths]
    out_shape = [jax.ShapeDtypeStruct((rows, wd), dt) for wd, dt in zip(widths, dtypes)]
    return pl.pallas_call(
        functools.partial(_norm_proj_kernel, n, tuple(widths)),
        grid=(rows // tm,), in_specs=in_specs, out_specs=out_specs, out_shape=out_shape,
        compiler_params=_cparams(("parallel",)), name="norm_proj",
    )(x, g, *weights)


def _gmlp_kernel(n_mix, c_rows, ua_ref, va_ref, lng_ref, lnb_ref, wmix_ref, bias_ref, ya_ref, vn_ref):
    u = jax.nn.gelu(ua_ref[...])
    gv = jax.nn.gelu(va_ref[...])
    mu = jnp.mean(gv, axis=-1, keepdims=True)
    xc = gv - mu
    var = jnp.mean(xc * xc, axis=-1, keepdims=True)
    vn = xc * lax.rsqrt(var + EPS) * lng_ref[...] + lnb_ref[...]
    vn_ref[...] = vn
    vnb = vn.astype(BF16)
    for c in range(n_mix):
        r0 = c * c_rows
        for g in range(GM_GROUPS):
            l0 = g * LANES
            mixed = jnp.dot(wmix_ref[g], vnb[r0:r0 + c_rows, l0:l0 + LANES], preferred_element_type=F32)
            mixed = mixed + bias_ref[:, l0:l0 + LANES]
            ya_ref[r0:r0 + c_rows, l0:l0 + LANES] = (u[r0:r0 + c_rows, l0:l0 + LANES] * mixed).astype(ya_ref.dtype)


def _gmlp(ua, va, ln_g, ln_b, wmix, bias, tm):
    rows = ua.shape[0]
    c_rows = wmix.shape[1]
    n_mix = tm // c_rows
    return pl.pallas_call(
        functools.partial(_gmlp_kernel, n_mix, c_rows),
        grid=(rows // tm,),
        in_specs=[pl.BlockSpec((tm, D_A), lambda i: (i, 0)), pl.BlockSpec((tm, D_A), lambda i: (i, 0)),
                  pl.BlockSpec((1, D_A), lambda i: (0, 0)), pl.BlockSpec((1, D_A), lambda i: (0, 0)),
                  pl.BlockSpec(wmix.shape, lambda i: (0, 0, 0)), pl.BlockSpec(bias.shape, lambda i: (0, 0))],
        out_specs=[pl.BlockSpec((tm, D_A), lambda i: (i, 0)), pl.BlockSpec((tm, D_A), lambda i: (i, 0))],
        out_shape=[jax.ShapeDtypeStruct((rows, D_A), BF16), jax.ShapeDtypeStruct((rows, D_A), F32)],
        compiler_params=_cparams(("parallel",)), name="gmlp",
    )(ua, va, ln_g, ln_b, wmix, bias)


def _dsa_prompt_kernel(topk, qi_ref, w_ref, q2_ref, ki_ref, k_ref, vt_ref, o_ref, sc_scr):
    blk = pl.program_id(1)
    nk = blk + 1
    qpos = blk * Q_BLOCK + lax.broadcasted_iota(I32, (Q_BLOCK, Q_BLOCK), 1)
    row = lax.broadcasted_iota(I32, (Q_BLOCK, Q_BLOCK), 0)
    w_row = w_ref[0, 0] * (IDX_HEADS ** -0.5)
    qi = qi_ref[0, 0]

    def score_chunk(c, carry):
        s = lax.dot_general(ki_ref[0, c], qi, (((1,), (1,)), ((), ())), preferred_element_type=F32)
        s = jnp.maximum(s, 0.0) * w_row
        sc = s[:, 0:LANES]
        for h in range(1, IDX_HEADS):
            sc = sc + s[:, h * LANES:(h + 1) * LANES]
        kpos = c * Q_BLOCK + row
        sc_scr[c] = jnp.where(kpos <= qpos, sc, NEG_INF)
        return carry

    lax.fori_loop(0, nk, score_chunk, 0)

    def count(pred):
        def body(c, acc):
            return acc + pred(sc_scr[c], c).astype(I32)
        acc = lax.fori_loop(0, nk, body, jnp.zeros((Q_BLOCK, Q_BLOCK), I32))
        return jnp.sum(acc, axis=0, keepdims=True)

    theta = _kth_largest(lambda t: count(lambda s, c: s >= t), topk, (1, Q_BLOCK))
    need = topk - count(lambda s, c: s > theta)
    pcut = _tie_cutoff(lambda p: count(lambda s, c: (s == theta) & (c * Q_BLOCK + row < p)), need, 11, (1, Q_BLOCK))

    q2 = q2_ref[0, 0]
    n_hq = N_HEADS * Q_BLOCK

    def attn_chunk(c, carry):
        m, l, acc = carry
        sc = sc_scr[c]
        kpos = c * Q_BLOCK + row
        sel = ((sc > theta) | ((sc == theta) & (kpos <= pcut) & (need > 0))) & (sc > NEG_INF)
        bias = jnp.where(sel, 0.0, NEG_INF)
        s = lax.dot_general(k_ref[0, c], q2, (((1,), (1,)), ((), ())), preferred_element_type=F32)
        s = s * (HEAD_DIM ** -0.5) + jnp.concatenate([bias] * N_HEADS, axis=1)
        m_new = jnp.maximum(m, jnp.max(s, axis=0, keepdims=True))
        m_safe = jnp.where(m_new == NEG_INF, 0.0, m_new)
        alpha = jnp.exp(m - m_safe)
        p = jnp.exp(s - m_safe)
        l = alpha * l + jnp.sum(p, axis=0, keepdims=True)
        acc = alpha * acc + jnp.dot(vt_ref[0, c], p.astype(BF16), preferred_element_type=F32)
        return m_new, l, acc

    m0 = jnp.full((1, n_hq), NEG_INF, F32)
    l0 = jnp.zeros((1, n_hq), F32)
    a0 = jnp.zeros((KV_W, n_hq), F32)
    _, l, acc = lax.fori_loop(0, nk, attn_chunk, (m0, l0, a0))
    o = acc / l
    half = n_hq // 2
    o_ref[0, 0] = jnp.concatenate([o[0:HEAD_DIM, 0:half], o[HEAD_DIM:KV_W, half:n_hq]], axis=1).astype(o_ref.dtype)


def _dsa_prompt(qi2, w_row, q2, ki4, k4, vt4, topk):
    b, nblk = qi2.shape[0], qi2.shape[1]
    nch = ki4.shape[1]
    n_hq = N_HEADS * Q_BLOCK
    return pl.pallas_call(
        functools.partial(_dsa_prompt_kernel, topk),
        grid=(b, nblk),
        in_specs=[pl.BlockSpec((1, 1, n_hq, IDX_DIM), lambda i, j: (i, j, 0, 0)),
                  pl.BlockSpec((1, 1, 1, n_hq), lambda i, j: (i, j, 0, 0)),
                  pl.BlockSpec((1, 1, n_hq, KV_W), lambda i, j: (i, j, 0, 0)),
                  pl.BlockSpec((1, nch, Q_BLOCK, IDX_DIM), lambda i, j: (i, 0, 0, 0)),
                  pl.BlockSpec((1, nch, Q_BLOCK, KV_W), lambda i, j: (i, 0, 0, 0)),
                  pl.BlockSpec((1, nch, KV_W, Q_BLOCK), lambda i, j: (i, 0, 0, 0))],
        out_specs=pl.BlockSpec((1, 1, HEAD_DIM, n_hq), lambda i, j: (i, j, 0, 0)),
        out_shape=jax.ShapeDtypeStruct((b, nblk, HEAD_DIM, n_hq), BF16),
        scratch_shapes=[pltpu.VMEM((nch, Q_BLOCK, Q_BLOCK), F32)],
        compiler_params=_cparams(("parallel", "arbitrary")), name="dsa_prompt",
    )(qi2, w_row, q2, ki4, k4, vt4)


def _dsa_sample_kernel(topk, n_pages, t_q, grp, pt_ref, qi_ref, w_ref, q2_ref, kin_ref, kn_ref, vn_ref,
                       cki_hbm, ck_hbm, cv_hbm, o_ref, ki_buf, k_buf, v_buf, sc_scr, bias_scr, sem):
    g0 = pl.program_id(0) * grp
    n_keys = (n_pages + 1) * PAGE_SIZE
    n_col = n_keys // LANES
    new0 = n_pages * PAGE_SIZE
    n_rows = t_q * IDX_HEADS

    def page_copy(hbm, buf, which, b, p, slot):
        win = pl.ds(pl.multiple_of(p * PAGE_SIZE, PAGE_SIZE), PAGE_SIZE)
        return pltpu.make_async_copy(hbm.at[pt_ref[g0 + b, p]], buf.at[slot, :, win], sem.at[which, slot])

    def start_ki(b, slot):
        lax.fori_loop(0, n_pages, lambda p, c: (page_copy(cki_hbm, ki_buf, 0, b, p, slot).start(), c)[1], 0)

    def wait_ki(b, slot):
        lax.fori_loop(0, n_pages, lambda p, c: (page_copy(cki_hbm, ki_buf, 0, b, p, slot).wait(), c)[1], 0)

    def start_kv(b, slot):
        def body(p, c):
            page_copy(ck_hbm, k_buf, 1, b, p, slot).start()
            page_copy(cv_hbm, v_buf, 2, b, p, slot).start()
            return c
        lax.fori_loop(0, n_pages, body, 0)

    def wait_kv(b, slot):
        def body(p, c):
            page_copy(ck_hbm, k_buf, 1, b, p, slot).wait()
            page_copy(cv_hbm, v_buf, 2, b, p, slot).wait()
            return c
        lax.fori_loop(0, n_pages, body, 0)

    start_ki(0, 0)
    start_kv(0, 0)
    lane = lax.broadcasted_iota(I32, (t_q, PAGE_SIZE), 1)
    qrow = lax.broadcasted_iota(I32, (t_q, PAGE_SIZE), 0)
    new_vis = lane <= qrow

    def score_pair(j, carry):
        rows = []
        for r in range(2):
            b = 2 * j + r
            slot = r

            @pl.when(b + 1 < grp)
            def _():
                start_ki(b + 1, 1 - slot)

            wait_ki(b, slot)
            ki_buf[slot, :, new0:n_keys] = kin_ref[b]
            s = jnp.dot(qi_ref[b], ki_buf[slot].astype(BF16), preferred_element_type=F32)
            s = jnp.maximum(s, 0.0) * (w_ref[b] * (IDX_HEADS ** -0.5))
            sc = jnp.sum(s.reshape(t_q, IDX_HEADS, n_keys), axis=1)
            rows.append(jnp.concatenate([sc[:, :new0], jnp.where(new_vis, sc[:, new0:], NEG_INF)], axis=1))
        sc_scr[pl.ds(pl.multiple_of(j * 2 * t_q, 2 * t_q), 2 * t_q), :] = jnp.concatenate(rows, axis=0)
        return carry

    lax.fori_loop(0, grp // 2, score_pair, 0)

    n_sel = grp * t_q
    pos = lax.broadcasted_iota(I32, (n_sel, LANES), 1)

    def count(pred):
        acc = jnp.zeros((n_sel, LANES), I32)
        for c in range(n_col):
            acc = acc + pred(sc_scr[:, c * LANES:(c + 1) * LANES], c).astype(I32)
        return jnp.sum(acc, axis=1, keepdims=True)

    theta = _kth_largest(lambda t: count(lambda s, c: s >= t), topk, (n_sel, 1))
    need = topk - count(lambda s, c: s > theta)
    pcut = _tie_cutoff(lambda p: count(lambda s, c: (s == theta) & (c * LANES + pos < p)), need, 14, (n_sel, 1))
    for c in range(n_col):
        sc = sc_scr[:, c * LANES:(c + 1) * LANES]
        sel = ((sc > theta) | ((sc == theta) & (c * LANES + pos <= pcut) & (need > 0))) & (sc > NEG_INF)
        bias_scr[:, c * LANES:(c + 1) * LANES] = jnp.where(sel, 0.0, NEG_INF)

    head = lax.broadcasted_iota(I32, (n_rows, HEAD_DIM), 0) % N_HEADS

    def attend(b, carry):
        slot = b % 2

        @pl.when(b + 1 < grp)
        def _():
            start_kv(b + 1, 1 - slot)

        wait_kv(b, slot)
        k_buf[slot, :, new0:n_keys] = kn_ref[b]
        v_buf[slot, :, new0:n_keys] = vn_ref[b]
        bias = jnp.concatenate(
            [jnp.broadcast_to(bias_scr[pl.ds(b * t_q + t, 1), :], (N_HEADS, n_keys)) for t in range(t_q)], axis=0)
        s = jnp.dot(q2_ref[b], k_buf[slot].astype(BF16), preferred_element_type=F32)
        s = s * (HEAD_DIM ** -0.5) + bias
        p = jnp.exp(s - jnp.max(s, axis=1, keepdims=True))
        l = jnp.sum(p, axis=1, keepdims=True)
        o = lax.dot_general(p.astype(BF16), v_buf[slot].astype(BF16), (((1,), (1,)), ((), ())),
                            preferred_element_type=F32) / l
        o_ref[b] = jnp.where(head < N_HEADS // N_KV_HEADS, o[:, 0:HEAD_DIM], o[:, HEAD_DIM:KV_W]).astype(o_ref.dtype)
        return carry

    lax.fori_loop(0, grp, attend, 0)


def _dsa_sample(page_table, qi_s, w_s, q2_s, ki_new, k_new, v_new, cki, ck, cv, topk, grp):
    db, n_pages = page_table.shape
    n_rows = qi_s.shape[1]
    t_q = n_rows // IDX_HEADS
    n_keys = (n_pages + 1) * PAGE_SIZE
    per_b = lambda *shape: pl.BlockSpec((grp,) + shape, lambda i, pt: (i,) + (0,) * len(shape))
    grid_spec = pltpu.PrefetchScalarGridSpec(
        num_scalar_prefetch=1, grid=(db // grp,),
        in_specs=[per_b(n_rows, IDX_DIM), per_b(n_rows, 1), per_b(n_rows, KV_W),
                  per_b(IDX_DIM, PAGE_SIZE), per_b(KV_W, PAGE_SIZE), per_b(KV_W, PAGE_SIZE),
                  pl.BlockSpec(memory_space=pl.ANY), pl.BlockSpec(memory_space=pl.ANY),
                  pl.BlockSpec(memory_space=pl.ANY)],
        out_specs=per_b(n_rows, HEAD_DIM),
        scratch_shapes=[pltpu.VMEM((2, IDX_DIM, n_keys), F32),
                        pltpu.VMEM((2, KV_W, n_keys), F32),
                        pltpu.VMEM((2, KV_W, n_keys), F32),
                        pltpu.VMEM((grp * t_q, n_keys), F32),
                        pltpu.VMEM((grp * t_q, n_keys), F32),
                        pltpu.SemaphoreType.DMA((3, 2))])
    return pl.pallas_call(
        functools.partial(_dsa_sample_kernel, topk, n_pages, t_q, grp),
        grid_spec=grid_spec,
        out_shape=jax.ShapeDtypeStruct((db, n_rows, HEAD_DIM), BF16),
        compiler_params=_cparams(("arbitrary",), VMEM_LIMIT_PAGED), name="dsa_sample",
    )(page_table, qi_s, w_s, q2_s, ki_new, k_new, v_new, cki, ck, cv)


def _mix_out(x_ref, ya_ref, yb_ref, woa_ref, wob_ref, gmem_ref, wq_ref):
    x1 = x_ref[...] + jnp.dot(ya_ref[...], woa_ref[...], preferred_element_type=F32)
    x1 = x1 + jnp.dot(yb_ref[...], wob_ref[...], preferred_element_type=F32)
    h = _rms(x1, gmem_ref[...]).astype(BF16)
    return x1, jnp.dot(h, wq_ref[...], preferred_element_type=F32).astype(BF16)


def _post_mix_kernel(x_ref, ya_ref, yb_ref, woa_ref, wob_ref, gmem_ref, wq_ref, mk_ref, mv_ref,
                     wo_ref, gffn_ref, x2_ref, h3_ref):
    x1, q = _mix_out(x_ref, ya_ref, yb_ref, woa_ref, wob_ref, gmem_ref, wq_ref)
    mk, mv = mk_ref[0].astype(BF16), mv_ref[0].astype(BF16)
    outs = []
    for h in range(MEM_HEADS):
        sl = slice(h * MEM_HEAD_DIM, (h + 1) * MEM_HEAD_DIM)
        s = lax.dot_general(q[:, sl], mk[:, sl], (((1,), (1,)), ((), ())), preferred_element_type=F32)
        s = s * (MEM_HEAD_DIM ** -0.5)
        p = jnp.exp(s - jnp.max(s, axis=-1, keepdims=True))
        p = p / jnp.sum(p, axis=-1, keepdims=True)
        outs.append(jnp.dot(p.astype(BF16), mv[:, sl], preferred_element_type=F32))
    o = jnp.concatenate(outs, axis=1)
    x2 = x1 + jnp.dot(o.astype(BF16), wo_ref[...], preferred_element_type=F32)
    x2_ref[...] = x2
    h3_ref[...] = _rms(x2, gffn_ref[...]).astype(BF16)


def _post_mix_sample_kernel(grp, t_q, x_ref, ya_ref, yb_ref, woa_ref, wob_ref, gmem_ref, wq_ref, mk_ref, mv_ref,
                            wo_ref, gffn_ref, x2_ref, h3_ref):
    x1, q = _mix_out(x_ref, ya_ref, yb_ref, woa_ref, wob_ref, gmem_ref, wq_ref)
    n_mr = N_MEM * MEM_HEADS
    n_qr = MEM_HEADS * t_q
    own = (lax.broadcasted_iota(I32, (n_qr, n_mr), 1) % MEM_HEADS) == (lax.broadcasted_iota(I32, (n_qr, n_mr), 0) // t_q)
    outs = []
    for b in range(grp):
        qb = q[b * t_q:(b + 1) * t_q]
        qh = jnp.concatenate([qb[:, h * MEM_HEAD_DIM:(h + 1) * MEM_HEAD_DIM] for h in range(MEM_HEADS)], axis=0)
        s = lax.dot_general(qh, mk_ref[b].astype(BF16), (((1,), (1,)), ((), ())), preferred_element_type=F32)
        s = jnp.where(own, s * (MEM_HEAD_DIM ** -0.5), NEG_INF)
        p = jnp.exp(s - jnp.max(s, axis=-1, keepdims=True))
        p = p / jnp.sum(p, axis=-1, keepdims=True)
        oh = jnp.dot(p.astype(BF16), mv_ref[b].astype(BF16), preferred_element_type=F32)
        outs.append(jnp.concatenate([oh[h * t_q:(h + 1) * t_q] for h in range(MEM_HEADS)], axis=1))
    o = jnp.concatenate(outs, axis=0)
    x2 = x1 + jnp.dot(o.astype(BF16), wo_ref[...], preferred_element_type=F32)
    x2_ref[...] = x2
    h3_ref[...] = _rms(x2, gffn_ref[...]).astype(BF16)


def _post_mix(x, ya, yb, woa, wob, gmem, wq, mk, mv, wo, gffn, tm, rows_per_mem):
    rows = x.shape[0]
    if rows_per_mem >= tm:
        per = rows_per_mem // tm
        body = _post_mix_kernel
        mem_spec = pl.BlockSpec((1, N_MEM, D_MEM), lambda i: (i // per, 0, 0))
    else:
        grp = tm // rows_per_mem
        body = functools.partial(_post_mix_sample_kernel, grp, rows_per_mem)
        mem_spec = pl.BlockSpec((grp, N_MEM * MEM_HEADS, MEM_HEAD_DIM), lambda i: (i, 0, 0))
    full = lambda a: pl.BlockSpec(a.shape, lambda i: (0,) * a.ndim)
    return pl.pallas_call(
        body,
        grid=(rows // tm,),
        in_specs=[pl.BlockSpec((tm, D_MODEL), lambda i: (i, 0)), pl.BlockSpec((tm, D_A), lambda i: (i, 0)),
                  pl.BlockSpec((tm, D_A), lambda i: (i, 0)), full(woa), full(wob), full(gmem), full(wq),
                  mem_spec, mem_spec, full(wo), full(gffn)],
        out_specs=[pl.BlockSpec((tm, D_MODEL), lambda i: (i, 0)), pl.BlockSpec((tm, D_MODEL), lambda i: (i, 0))],
        out_shape=[jax.ShapeDtypeStruct((rows, D_MODEL), F32), jax.ShapeDtypeStruct((rows, D_MODEL), BF16)],
        compiler_params=_cparams(("parallel",)), name="post_mix",
    )(x, ya, yb, woa, wob, gmem, wq, mk, mv, wo, gffn)


def _top_values(s, n):
    vals = []
    for _ in range(n):
        m = jnp.max(s, axis=0, keepdims=True)
        vals.append(m)
        s = jnp.where(s == m, NEG_INF, s)
    return vals


def _peer_gate_kernel(h_ref, wq_ref, keys_ref, r1_ref, e1_ref, n0_ref, a0_ref):
    q = jnp.dot(h_ref[...], wq_ref[...], preferred_element_type=F32).astype(BF16)
    st = lax.dot_general(keys_ref[...], q, (((1,), (1,)), ((), ())), preferred_element_type=F32)
    for h in range(PK_HEADS):
        s0 = st[(2 * h) * N_KEYS:(2 * h + 1) * N_KEYS]
        s1 = st[(2 * h + 1) * N_KEYS:(2 * h + 2) * N_KEYS]
        a = _top_values(s0, PK_TOPK)
        b = _top_values(s1, PK_TOPK)
        cand = [a[k] + b[l] for k in range(PK_TOPK) for l in range(PK_TOPK) if (k + 1) * (l + 1) <= PK_TOPK]
        top = _top_values(jnp.concatenate(cand, axis=0), PK_TOPK)
        theta = top[PK_TOPK - 1]
        z = jnp.ones_like(theta)
        for r in range(1, PK_TOPK):
            z = z + jnp.exp(top[r] - top[0])
        rank1 = jnp.zeros_like(s1)
        n0 = jnp.zeros_like(s0)
        for l in range(PK_TOPK):
            rank1 = rank1 + jnp.where(s1 < b[l], 1.0, 0.0)
            n0 = n0 + jnp.where(s0 + b[l] >= theta, 1.0, 0.0)
        r1_ref[h] = rank1.astype(BF16)
        e1_ref[h] = jnp.exp(s1 - b[0]).astype(BF16)
        n0_ref[h] = n0
        a0_ref[h] = jnp.exp(s0 - a[0]) / z


def _peer_gate(h3, wq, keys_pad, tt):
    rows = h3.shape[0]
    big = lambda dt: jax.ShapeDtypeStruct((PK_HEADS, N_KEYS, rows), dt)
    bspec = pl.BlockSpec((PK_HEADS, N_KEYS, tt), lambda i: (0, 0, i))
    return pl.pallas_call(
        _peer_gate_kernel,
        grid=(rows // tt,),
        in_specs=[pl.BlockSpec((tt, D_MODEL), lambda i: (i, 0)), pl.BlockSpec(wq.shape, lambda i: (0, 0)),
                  pl.BlockSpec(keys_pad.shape, lambda i: (0, 0))],
        out_specs=[bspec, bspec, bspec, bspec],
        out_shape=[big(BF16), big(BF16), big(F32), big(F32)],
        compiler_params=_cparams(("parallel",)), name="peer_gate",
    )(h3, wq, keys_pad)


def _peer_dense_kernel(n_i, h_ref, x_ref, u_ref, v_ref, r1_ref, e1_ref, n0_ref, a0_ref, gfin_ref, y_ref, acc_ref):
    e = pl.program_id(1)
    tt = h_ref.shape[0]
    n_pk = N_KEYS // BF16_ROWS

    @pl.when(e == 0)
    def _():
        acc_ref[...] = jnp.zeros_like(acc_ref)

    at = lax.dot_general(u_ref[...], h_ref[...], (((1,), (1,)), ((), ())), preferred_element_type=F32)
    at = jax.nn.gelu(at)
    ws = []
    for ii in range(n_i):
        i = e * n_i + ii
        g = None
        for h in range(PK_HEADS):
            n_row = jnp.broadcast_to(n0_ref[h, pl.ds(i, 1), :], (BF16_ROWS, tt)).astype(BF16)
            a_row = jnp.broadcast_to(a0_ref[h, pl.ds(i, 1), :], (BF16_ROWS, tt)).astype(BF16)
            r1 = r1_ref[h].reshape(n_pk, BF16_ROWS, tt)
            e1 = e1_ref[h].reshape(n_pk, BF16_ROWS, tt)
            gh = jnp.where(r1 < n_row[None], e1 * a_row[None], jnp.zeros((), BF16))
            g = gh if g is None else g + gh
        ws.append(at[ii * N_KEYS:(ii + 1) * N_KEYS].astype(BF16) * g.reshape(N_KEYS, tt))
    wt = jnp.concatenate(ws, axis=0) if n_i > 1 else ws[0]
    acc_ref[...] += lax.dot_general(wt, v_ref[...], (((0,), (0,)), ((), ())), preferred_element_type=F32)

    @pl.when(e == pl.num_programs(1) - 1)
    def _():
        y_ref[...] = _rms(x_ref[...] + acc_ref[...], gfin_ref[...])


def _peer_dense(h3, x2, u, v, r1, e1, n0, a0, gfin, tt, eb):
    rows = h3.shape[0]
    n_exp = u.shape[0]
    n_i = eb // N_KEYS
    gspec = pl.BlockSpec((PK_HEADS, N_KEYS, tt), lambda t, e: (0, 0, t))
    return pl.pallas_call(
        functools.partial(_peer_dense_kernel, n_i),
        grid=(rows // tt, n_exp // eb),
        in_specs=[pl.BlockSpec((tt, D_MODEL), lambda t, e: (t, 0)), pl.BlockSpec((tt, D_MODEL), lambda t, e: (t, 0)),
                  pl.BlockSpec((eb, D_MODEL), lambda t, e: (e, 0)), pl.BlockSpec((eb, D_MODEL), lambda t, e: (e, 0)),
                  gspec, gspec, gspec, gspec, pl.BlockSpec((1, D_MODEL), lambda t, e: (0, 0))],
        out_specs=pl.BlockSpec((tt, D_MODEL), lambda t, e: (t, 0)),
        out_shape=jax.ShapeDtypeStruct((rows, D_MODEL), F32),
        scratch_shapes=[pltpu.VMEM((tt, D_MODEL), F32)],
        compiler_params=_cparams(("parallel", "arbitrary")), name="peer_dense",
    )(h3, x2, u, v, r1, e1, n0, a0, gfin)


def _heads_major(a, nblk, blk, heads, dim):
    b = a.shape[0] // (nblk * blk)
    return a.reshape(b, nblk, blk, heads, dim).transpose(0, 1, 3, 2, 4)


def _pad_kv_groups(q5):
    z = jnp.zeros_like(q5)
    half = N_HEADS // N_KV_HEADS
    lo = jnp.concatenate([q5[..., :half, :, :], z[..., :half, :, :]], axis=-1)
    hi = jnp.concatenate([z[..., half:, :, :], q5[..., half:, :, :]], axis=-1)
    return jnp.concatenate([lo, hi], axis=-3)


def kernel(x_prompt, x_sample, cache_k, cache_v, cache_kidx, cache_mem_k, cache_mem_v, page_table, mem_prompt,
           norm_mix_g, w_in, gm_ln_g, gm_ln_b, gm_ws, gm_bs, w_out, norm_mem_g, mem_norm_g, mem_wq, mem_wkv, mem_wo,
           norm_ffn_g, peer_wq, peer_keys, peer_u, peer_v, final_norm_g):
    assert w_in.shape[0] == 1, "one layer"
    B, S, D = x_prompt.shape
    DB, T, _ = x_sample.shape
    nblk = S // Q_BLOCK
    xp = x_prompt.reshape(B * S, D)
    xs = x_sample.reshape(DB * T, D)
    row = lambda a: a.reshape(1, -1)

    sizes = (D_A, D_A, N_HEADS * HEAD_DIM, KV_W, KV_W, IDX_HEADS * IDX_DIM, IDX_DIM, IDX_HEADS)
    offs = [0]
    for s_ in sizes:
        offs.append(offs[-1] + s_)
    w_parts = [w_in[0][:, offs[i]:offs[i + 1]].astype(BF16) for i in range(len(sizes))]
    w_parts[-1] = jnp.pad(w_parts[-1], ((0, 0), (0, LANES - IDX_HEADS)))
    dts = (F32, F32, BF16, F32, F32, BF16, F32, F32)
    g_mix = row(norm_mix_g[0])
    ua, va, q, k, v, qi, ki, wi = _norm_proj(xp, g_mix, w_parts, sizes, dts, tm=512)
    ua_s, va_s, q_s, k_s, v_s, qi_s, ki_s, wi_s = _norm_proj(xs, g_mix, w_parts, sizes, dts, tm=DB * T)

    tril = jnp.tril(jnp.ones((CHUNK, CHUNK), bool))
    wmix_p = jnp.where(tril, gm_ws[0], 0).astype(BF16)
    bias_p = jnp.repeat(gm_bs[0].T, LANES, axis=1)
    ya, vn_p = _gmlp(ua, va, row(gm_ln_g[0]), row(gm_ln_b[0]), wmix_p, bias_p, tm=512)
    w_small = jnp.where(tril[:T, :T], gm_ws[0][:, :T, :T], 0)
    wmix_s = jnp.einsum("ab,gts->gatbs", jnp.eye(DB, dtype=F32), w_small).reshape(GM_GROUPS, DB * T, DB * T)
    bias_s = jnp.tile(jnp.repeat(gm_bs[0][:, :T].T, LANES, axis=1), (DB, 1))
    ya_s, vn_s = _gmlp(ua_s, va_s, row(gm_ln_g[0]), row(gm_ln_b[0]), wmix_s.astype(BF16), bias_s, tm=DB * T)

    topk_p = min(TOPK_MAX, S // 4)
    n_hq = N_HEADS * Q_BLOCK
    qi2 = _heads_major(qi, nblk, Q_BLOCK, IDX_HEADS, IDX_DIM).reshape(B, nblk, n_hq, IDX_DIM)
    w_rowp = wi.reshape(B, nblk, Q_BLOCK, IDX_HEADS).transpose(0, 1, 3, 2).reshape(B, nblk, 1, n_hq)
    q2 = _pad_kv_groups(_heads_major(q, nblk, Q_BLOCK, N_HEADS, HEAD_DIM)).reshape(B, nblk, n_hq, KV_W)
    ki4 = ki.astype(BF16).reshape(B, nblk, Q_BLOCK, IDX_DIM)
    k4 = k.astype(BF16).reshape(B, nblk, Q_BLOCK, KV_W)
    vt4 = v.astype(BF16).reshape(B, nblk, Q_BLOCK, KV_W).transpose(0, 1, 3, 2)
    o_t = _dsa_prompt(qi2, w_rowp, q2, ki4, k4, vt4, topk_p)
    yb = o_t.reshape(B, nblk, HEAD_DIM, N_HEADS, Q_BLOCK).transpose(0, 1, 4, 3, 2).reshape(B * S, N_HEADS * HEAD_DIM)

    n_pages = page_table.shape[1]
    topk_s = min(TOPK_MAX, (n_pages * PAGE_SIZE + T) // 4)
    n_pool = cache_k.shape[1]
    qi_sr = qi_s.reshape(DB, T * IDX_HEADS, IDX_DIM)
    w_sr = wi_s.reshape(DB, T * IDX_HEADS, 1)
    q5_s = q_s.reshape(DB, T, N_HEADS, HEAD_DIM).transpose(0, 2, 1, 3)
    q2_s = _pad_kv_groups(q5_s).transpose(0, 2, 1, 3).reshape(DB, T * N_HEADS, KV_W)
    new_page = lambda a: jnp.pad(a.reshape(DB, T, -1), ((0, 0), (0, PAGE_SIZE - T), (0, 0))).transpose(0, 2, 1)
    o_s = _dsa_sample(page_table, qi_sr, w_sr, q2_s, new_page(ki_s), new_page(k_s), new_page(v_s),
                      cache_kidx[0].transpose(0, 2, 1),
                      cache_k[0].transpose(0, 2, 3, 1).reshape(n_pool, KV_W, PAGE_SIZE),
                      cache_v[0].transpose(0, 2, 3, 1).reshape(n_pool, KV_W, PAGE_SIZE), topk_s, grp=8)
    yb_s = o_s.reshape(DB * T, N_HEADS * HEAD_DIM)

    wkv = mem_wkv[0].astype(BF16)
    mk_p, mv_p = _norm_proj(mem_prompt.reshape(B * N_MEM, D), row(mem_norm_g[0]), [wkv[:, :D_MEM], wkv[:, D_MEM:]],
                            (D_MEM, D_MEM), (F32, F32), tm=512)

    woa, wob = w_out[0][:D_A].astype(BF16), w_out[0][D_A:].astype(BF16)
    post = functools.partial(_post_mix, woa=woa, wob=wob, gmem=row(norm_mem_g[0]), wq=mem_wq[0].astype(BF16),
                             wo=mem_wo[0].astype(BF16), gffn=row(norm_ffn_g[0]))
    x2_p, h3_p = post(xp, ya, yb, mk=mk_p.reshape(B, N_MEM, D_MEM), mv=mv_p.reshape(B, N_MEM, D_MEM),
                      tm=512, rows_per_mem=S)
    x2_s, h3_s = post(xs, ya_s, yb_s, mk=cache_mem_k[0].reshape(DB, N_MEM * MEM_HEADS, MEM_HEAD_DIM),
                      mv=cache_mem_v[0].reshape(DB, N_MEM * MEM_HEADS, MEM_HEAD_DIM), tm=32, rows_per_mem=T)

    x2 = jnp.concatenate([x2_p, x2_s], axis=0)
    h3 = jnp.concatenate([h3_p, h3_s], axis=0)
    keys = peer_keys[0].reshape(PK_HEADS * 2, N_KEYS, KEY_HALF)
    eye = jnp.eye(PK_HEADS * 2, dtype=F32)
    keys_pad = jnp.einsum("gnd,gf->gnfd", keys, eye).reshape(PK_HEADS * 2 * N_KEYS, PK_HEADS * 2 * KEY_HALF)
    r1, e1, n0, a0 = _peer_gate(h3, peer_wq[0].astype(BF16), keys_pad.astype(BF16), tt=512)
    y = _peer_dense(h3, x2, peer_u[0].astype(BF16), peer_v[0].astype(BF16), r1, e1, n0, a0,
                    row(final_norm_g), tt=512, eb=512)

    y_prompt = y[:B * S].reshape(B, S, D)
    y_sample = y[B * S:].reshape(DB, T, D)
    shp = lambda a, *s: a.reshape(1, *s)
    return (y_prompt, y_sample,
            shp(k, B, S, N_KV_HEADS, HEAD_DIM), shp(v, B, S, N_KV_HEADS, HEAD_DIM), shp(ki, B, S, IDX_DIM),
            shp(vn_p.reshape(B, S, D_A)[:, S - CHUNK:], B, CHUNK, D_A),
            shp(mk_p, B, N_MEM, MEM_HEADS, MEM_HEAD_DIM), shp(mv_p, B, N_MEM, MEM_HEADS, MEM_HEAD_DIM),
            shp(k_s, DB, T, N_KV_HEADS, HEAD_DIM), shp(v_s, DB, T, N_KV_HEADS, HEAD_DIM), shp(ki_s, DB, T, IDX_DIM),
            shp(vn_s, DB, T, D_A))
```
